```python
import jax
import jax.numpy as jnp
from jax import lax
import numpy as np

D_MODEL = 2048
BATCH = 1
SEQ = 8192
DEPTH = 1
DEC_BATCH = 32
DEC_SEQ = 1
PAST_LEN = 8192
PAGE_SIZE = 128

N_META = 16
POOL_WINDOWS = (2, 4, 8, 16)
N_POOL_GROUPS = len(POOL_WINDOWS)
POOL_W = D_MODEL // 2
POOL_GROUP = POOL_W // N_POOL_GROUPS
POOL_PAD = max(POOL_WINDOWS) - 1
ATTN_W = D_MODEL // 2
HEAD_DIM = 128
N_HEADS = ATTN_W // HEAD_DIM
N_KV_HEADS = 2
GQA_GROUP = N_HEADS // N_KV_HEADS
N_IDX_HEADS = 16
IDX_DIM = 64
TOPK_MAX = 256
Q_BLOCK = 128
D_MIX = POOL_W + ATTN_W
SPLIT_WIDTHS = (POOL_W, POOL_W, N_HEADS * HEAD_DIM, N_KV_HEADS * HEAD_DIM, N_KV_HEADS * HEAD_DIM,
                ATTN_W, N_IDX_HEADS * IDX_DIM, IDX_DIM, N_IDX_HEADS)
D_IN_PROJ = sum(SPLIT_WIDTHS)
RMS_EPS = 1e-6

kernel_name = 'hymba_pool_dsa_alibi_step'


def rmsnorm(x, g):
    xf = x.astype(jnp.float32)
    y = xf * lax.rsqrt(jnp.mean(xf * xf, axis=-1, keepdims=True) + RMS_EPS)
    return (y * g.astype(jnp.float32)).astype(x.dtype)


def alibi_slopes():
    i = jnp.arange(1, N_HEADS + 1, dtype=jnp.float32)
    return jnp.exp2(-8.0 * i / N_HEADS).reshape(N_KV_HEADS, GQA_GROUP)


def topk_count(n_keys):
    return max(1, min(TOPK_MAX, n_keys // 4))


def in_projection(h, norm_g, w_in):
    B, T, _ = h.shape
    hn = rmsnorm(h, norm_g)
    cuts = np.cumsum(SPLIT_WIDTHS)[:-1].tolist()
    u, z_pool, q, k, v, z_attn, qi, ki, wi = jnp.split(hn @ w_in, cuts, axis=-1)
    q = q.reshape(B, T, N_KV_HEADS, GQA_GROUP, HEAD_DIM)
    k = k.reshape(B, T, N_KV_HEADS, HEAD_DIM)
    v = v.reshape(B, T, N_KV_HEADS, HEAD_DIM)
    qi = qi.reshape(B, T, N_IDX_HEADS, IDX_DIM)
    return u, z_pool, q, k, v, z_attn, qi, ki, wi


def pool_branch(u_prefix, u, z, pos, w_pool, pool_scale):
    B, T, _ = u.shape
    u_ext = jnp.concatenate([u_prefix.astype(u.dtype), u], axis=1)
    cs = jnp.cumsum(u_ext.astype(jnp.float32), axis=1)
    cs = jnp.concatenate([jnp.zeros((B, 1, POOL_W), jnp.float32), cs], axis=1)
    end = cs[:, POOL_PAD + 1:]
    means = []
    for g, w in enumerate(POOL_WINDOWS):
        sl = slice(g * POOL_GROUP, (g + 1) * POOL_GROUP)
        start = cs[:, POOL_PAD + 1 - w: POOL_PAD + 1 - w + T, sl]
        cnt = jnp.minimum(pos + 1, w).astype(jnp.float32)[None, :, None]
        means.append((end[..., sl] - start) / cnt)
    mixed = (jnp.concatenate(means, axis=-1) - u.astype(jnp.float32)).astype(u.dtype)
    mixed = jnp.einsum('btgc,gcd->btgd', mixed.reshape(B, T, N_POOL_GROUPS, POOL_GROUP), w_pool)
    out = mixed.reshape(B, T, POOL_W) * pool_scale * jax.nn.silu(z)
    return out, u_ext[:, -POOL_PAD:]


def indexer_select(qi, wi, ki, q_pos, n_sel):
    S = ki.shape[1]
    logits = jnp.einsum('bthd,bsd->bths', qi, ki).astype(jnp.float32) * IDX_DIM ** -0.5
    score = jnp.einsum('bth,bths->bts', wi.astype(jnp.float32) * N_IDX_HEADS ** -0.5, jax.nn.relu(logits))
    visible = jnp.arange(S)[None, :] <= q_pos[:, None]
    score = jnp.where(visible[None], score, -jnp.inf)
    vals, idx = lax.top_k(score, n_sel)
    return idx, jnp.isfinite(vals)


def sparse_attend(q, k_sel, v_sel, q_pos, idx, valid):
    B, T = q.shape[:2]
    s = jnp.einsum('btgrd,btkgd->btgrk', q, k_sel).astype(jnp.float32) * HEAD_DIM ** -0.5
    dist = (q_pos[None, :, None] - idx).astype(jnp.float32)
    s = s - alibi_slopes()[None, None, :, :, None] * dist[:, :, None, None, :]
    s = jnp.where(valid[:, :, None, None, :], s, -jnp.inf)
    p = jax.nn.softmax(s, axis=-1).astype(v_sel.dtype)
    o = jnp.einsum('btgrk,btkgd->btgrd', p, v_sel)
    return o.reshape(B, T, ATTN_W)


def prompt_attention(q, k, v, qi, ki, wi):
    B, L = k.shape[:2]
    n_sel = topk_count(L)
    n_blk = -(-L // Q_BLOCK)
    pad = n_blk * Q_BLOCK - L

    def blocks(a):
        a = jnp.pad(a, [(0, 0), (0, pad)] + [(0, 0)] * (a.ndim - 2))
        return jnp.moveaxis(a.reshape((B, n_blk, Q_BLOCK) + a.shape[2:]), 1, 0)

    pos_blocks = jnp.arange(n_blk * Q_BLOCK).reshape(n_blk, Q_BLOCK)
    bidx = jnp.arange(B)[:, None, None]

    def one_block(args):
        qb, qib, wib, pb = args
        idx, valid = indexer_select(qib, wib, ki, pb, n_sel)
        return sparse_attend(qb, k[bidx, idx], v[bidx, idx], pb, idx, valid)

    o = lax.map(one_block, (blocks(q), blocks(qi), blocks(wi), pos_blocks))
    return jnp.moveaxis(o, 0, 1).reshape(B, n_blk * Q_BLOCK, ATTN_W)[:, :L]


def sample_attention(q, k, v, qi, ki, wi, q_pos, cache_k, cache_v, cache_idx_k, page_table):
    B, T = k.shape[:2]
    past = page_table.shape[1] * PAGE_SIZE
    ki_past = cache_idx_k[page_table].reshape(B, past, IDX_DIM).astype(ki.dtype)
    ki_all = jnp.concatenate([ki_past, ki], axis=1)
    idx, valid = indexer_select(qi, wi, ki_all, q_pos, topk_count(past + T))
    bidx = jnp.arange(B)[:, None, None]
    in_past = idx < past
    pidx = jnp.minimum(idx, past - 1)
    phys = page_table[bidx, pidx // PAGE_SIZE]
    off = pidx % PAGE_SIZE
    nidx = jnp.clip(idx - past, 0, T - 1)

    def pick(cache, new):
        return jnp.where(in_past[..., None, None], cache[phys, off].astype(new.dtype), new[bidx, nidx])

    return sparse_attend(q, pick(cache_k, k), pick(cache_v, v), q_pos, idx, valid)


def merge_residual(h, o_pool, o_attn, z_attn, w_out):
    return h + jnp.concatenate([o_pool, o_attn * jax.nn.silu(z_attn)], axis=-1) @ w_out


def setup_inputs(seed: int = 0) -> dict:
    key = jax.random.key(seed)
    ks = jax.random.split(key, 14)
    n_pages = PAST_LEN // PAGE_SIZE
    n_used = DEC_BATCH * n_pages
    n_phys = n_used + max(1, n_used // 4)
    nrm = jax.random.normal
    page_table = jax.random.permutation(ks[0], n_phys)[:n_used].reshape(DEC_BATCH, n_pages).astype(jnp.int32)
    return {
        'x_prompt': nrm(ks[1], (BATCH, SEQ, D_MODEL), jnp.float32),
        'x_sample': nrm(ks[2], (DEC_BATCH, DEC_SEQ, D_MODEL), jnp.float32),
        'cache_k': nrm(ks[3], (DEPTH, n_phys, PAGE_SIZE, N_KV_HEADS, HEAD_DIM), jnp.float32),
        'cache_v': nrm(ks[4], (DEPTH, n_phys, PAGE_SIZE, N_KV_HEADS, HEAD_DIM), jnp.float32),
        'cache_idx_k': nrm(ks[5], (DEPTH, n_phys, PAGE_SIZE, IDX_DIM), jnp.float32),
        'state_pool': nrm(ks[6], (DEPTH, DEC_BATCH, POOL_PAD, POOL_W), jnp.float32),
        'page_table': page_table,
        'meta_tokens': nrm(ks[7], (N_META, D_MODEL), jnp.float32),
        'norm_g': 1.0 + 0.1 * nrm(ks[8], (DEPTH, D_MODEL), jnp.float32),
        'w_in': nrm(ks[9], (DEPTH, D_MODEL, D_IN_PROJ), jnp.float32) * D_MODEL ** -0.5,
        'w_pool': nrm(ks[10], (DEPTH, N_POOL_GROUPS, POOL_GROUP, POOL_GROUP), jnp.float32) * POOL_GROUP ** -0.5,
        'pool_scale': 1.0 + 0.1 * nrm(ks[11], (DEPTH, POOL_W), jnp.float32),
        'w_out': nrm(ks[12], (DEPTH, D_MIX, D_MODEL), jnp.float32) * D_MIX ** -0.5,
        'final_norm_g': 1.0 + 0.1 * nrm(ks[13], (D_MODEL,), jnp.float32),
    }


def reference(x_prompt, x_sample, cache_k, cache_v, cache_idx_k, state_pool, page_table,
              meta_tokens, norm_g, w_in, w_pool, pool_scale, w_out, final_norm_g):
    B, S, _ = x_prompt.shape
    T = x_sample.shape[1]
    past = page_table.shape[1] * PAGE_SIZE
    meta = jnp.broadcast_to(meta_tokens.astype(x_prompt.dtype)[None], (B, N_META, D_MODEL))
    hp = jnp.concatenate([meta, x_prompt], axis=1)
    hs = x_sample
    pos_p = jnp.arange(N_META + S)
    pos_s = past + jnp.arange(T)
    kp, vp, kip, pp, ksm, vsm, kism, psm = ([] for _ in range(8))
    for l in range(DEPTH):
        u, zp, q, k, v, za, qi, ki, wi = in_projection(hp, norm_g[l], w_in[l])
        o_pool, st = pool_branch(jnp.zeros((B, POOL_PAD, POOL_W), u.dtype), u, zp, pos_p, w_pool[l], pool_scale[l])
        o_attn = prompt_attention(q, k, v, qi, ki, wi)
        hp = merge_residual(hp, o_pool, o_attn, za, w_out[l])
        kp.append(k)
        vp.append(v)
        kip.append(ki)
        pp.append(st)
        u, zp, q, k, v, za, qi, ki, wi = in_projection(hs, norm_g[l], w_in[l])
        o_pool, st = pool_branch(state_pool[l], u, zp, pos_s, w_pool[l], pool_scale[l])
        o_attn = sample_attention(q, k, v, qi, ki, wi, pos_s, cache_k[l], cache_v[l], cache_idx_k[l], page_table)
        hs = merge_residual(hs, o_pool, o_attn, za, w_out[l])
        ksm.append(k)
        vsm.append(v)
        kism.append(ki)
        psm.append(st)
    y_prompt = rmsnorm(hp, final_norm_g)[:, N_META:]
    y_sample = rmsnorm(hs, final_norm_g)
    return (y_prompt, y_sample, jnp.stack(kp), jnp.stack(vp), jnp.stack(kip), jnp.stack(pp),
            jnp.stack(ksm), jnp.stack(vsm), jnp.stack(kism), jnp.stack(psm))
```

```python
import functools

import jax
import jax.numpy as jnp
from jax import lax
from jax.experimental import pallas as pl
from jax.experimental.pallas import tpu as pltpu

F32 = jnp.float32
BF16 = jnp.bfloat16
I32 = jnp.int32

N_META = 16
POOL_WINDOWS = (2, 4, 8, 16)
POOL_PAD = max(POOL_WINDOWS) - 1
HEAD_DIM = 128
N_KV_HEADS = 2
GQA_GROUP = 4
N_HEADS = N_KV_HEADS * GQA_GROUP
N_IDX_HEADS = 16
IDX_DIM = 64
TOPK_MAX = 256
PAGE_SIZE = 128
RMS_EPS = 1e-6

INT_MIN = -(2 ** 31)
NEG_BIG = -1e30

VMEM_LIMIT = 56 * 1024 * 1024
PROJ_TN = 512
ATT_TQ = 256
ATT_SK = 512
IDX_SUB = 64
PAGES_PER_STEP = 8


def _silu(x):
    return x * jax.nn.sigmoid(x)


def _sortable_key(x):
    bits = lax.bitcast_convert_type(x, I32)
    return bits ^ ((bits >> 31) & 0x7FFFFFFF)


def _proj_kernel(x_ref, g_ref, w_ref, o_ref, hn_ref):
    @pl.when(pl.program_id(1) == 0)
    def _():
        x = x_ref[...]
        ms = jnp.mean(x * x, axis=-1, keepdims=True)
        hn_ref[...] = ((x * lax.rsqrt(ms + RMS_EPS)) * g_ref[...]).astype(BF16)

    o_ref[...] = jnp.dot(hn_ref[...], w_ref[...], preferred_element_type=F32)


def _rmsnorm_matmul(x, g, w_bf, tm):
    m, d = x.shape
    n = w_bf.shape[1]
    return pl.pallas_call(
        _proj_kernel,
        grid=(m // tm, n // PROJ_TN),
        in_specs=[
            pl.BlockSpec((tm, d), lambda i, j: (i, 0)),
            pl.BlockSpec((1, d), lambda i, j: (0, 0)),
            pl.BlockSpec((d, PROJ_TN), lambda i, j: (0, j)),
        ],
        out_specs=pl.BlockSpec((tm, PROJ_TN), lambda i, j: (i, j)),
        out_shape=jax.ShapeDtypeStruct((m, n), F32),
        scratch_shapes=[pltpu.VMEM((tm, d), BF16)],
        compiler_params=pltpu.CompilerParams(
            dimension_semantics=("parallel", "arbitrary"), vmem_limit_bytes=VMEM_LIMIT),
        name="in_proj",
    )(x, g, w_bf)


def _proj_f32_kernel(x_ref, g_ref, w_ref, o_ref):
    x = x_ref[...]
    ms = jnp.mean(x * x, axis=-1, keepdims=True)
    hn = (x * lax.rsqrt(ms + RMS_EPS)) * g_ref[...]
    o_ref[...] = jnp.dot(hn, w_ref[...], precision=lax.Precision.HIGHEST, preferred_element_type=F32)


def _rmsnorm_matmul_f32(x, g, w, tn):
    m, d = x.shape
    n = w.shape[1]
    return pl.pallas_call(
        _proj_f32_kernel,
        grid=(n // tn,),
        in_specs=[
            pl.BlockSpec((m, d), lambda j: (0, 0)),
            pl.BlockSpec((1, d), lambda j: (0, 0)),
            pl.BlockSpec((d, tn), lambda j: (0, j)),
        ],
        out_specs=pl.BlockSpec((m, tn), lambda j: (0, j)),
        out_shape=jax.ShapeDtypeStruct((m, n), F32),
        compiler_params=pltpu.CompilerParams(
            dimension_semantics=("parallel",), vmem_limit_bytes=VMEM_LIMIT),
        name="in_proj_f32",
    )(x, g, w)


def _pool_kernel(u_ref, z_ref, wp_ref, sc_ref, o_ref, ext_ref, *, tm, pool_group):
    i = pl.program_id(0)

    @pl.when(i == 0)
    def _():
        ext_ref[0:16, :] = jnp.zeros((16, ext_ref.shape[1]), F32)

    ext_ref[16:16 + tm, :] = u_ref[...]
    pos = i * tm + lax.broadcasted_iota(I32, (tm, 1), 0)
    for g, w in enumerate(POOL_WINDOWS):
        cols = slice(g * pool_group, (g + 1) * pool_group)
        u = u_ref[:, cols]
        s = u
        for d in range(1, w):
            s = s + ext_ref[16 - d:16 - d + tm, cols]
        cnt = jnp.minimum(pos + 1, w).astype(F32)
        mixed = (s / cnt - u).astype(BF16)
        y = jnp.dot(mixed, wp_ref[g], preferred_element_type=F32)
        o_ref[:, cols] = (y * sc_ref[:, cols] * _silu(z_ref[:, cols])).astype(BF16)
    ext_ref[0:16, :] = ext_ref[tm:tm + 16, :]


def _pool_branch(proj, wp_bf, scale, tm):
    m = proj.shape[0]
    pool_w = scale.shape[1]
    pool_group = pool_w // len(POOL_WINDOWS)
    return pl.pallas_call(
        functools.partial(_pool_kernel, tm=tm, pool_group=pool_group),
        grid=(m // tm,),
        in_specs=[
            pl.BlockSpec((tm, pool_w), lambda i: (i, 0)),
            pl.BlockSpec((tm, pool_w), lambda i: (i, 1)),
            pl.BlockSpec(wp_bf.shape, lambda i: (0, 0, 0)),
            pl.BlockSpec((1, pool_w), lambda i: (0, 0)),
        ],
        out_specs=pl.BlockSpec((tm, pool_w), lambda i: (i, 0)),
        out_shape=jax.ShapeDtypeStruct((m, pool_w), BF16),
        scratch_shapes=[pltpu.VMEM((tm + 16, pool_w), F32)],
        compiler_params=pltpu.CompilerParams(
            dimension_semantics=("arbitrary",), vmem_limit_bytes=VMEM_LIMIT),
        name="pool_prompt",
    )(proj, proj, wp_bf, scale)


def _pool_sample_kernel(u_ref, z_ref, st_ref, wp_ref, sc_ref, o_ref, *, pool_group):
    for g, w in enumerate(POOL_WINDOWS):
        cols = slice(g * pool_group, (g + 1) * pool_group)
        u = u_ref[:, cols]
        s = u
        for d in range(1, w):
            s = s + st_ref[POOL_PAD - d][:, cols]
        mixed = (s / float(w) - u).astype(BF16)
        y = jnp.dot(mixed, wp_ref[g], preferred_element_type=F32)
        o_ref[:, cols] = (y * sc_ref[:, cols] * _silu(z_ref[:, cols])).astype(BF16)


def _pool_branch_sample(proj_s, state_t, wp_bf, scale):
    b = proj_s.shape[0]
    pool_w = scale.shape[1]
    pool_group = pool_w // len(POOL_WINDOWS)
    return pl.pallas_call(
        functools.partial(_pool_sample_kernel, pool_group=pool_group),
        grid=(1,),
        in_specs=[
            pl.BlockSpec((b, pool_w), lambda i: (0, 0)),
            pl.BlockSpec((b, pool_w), lambda i: (0, 1)),
            pl.BlockSpec(state_t.shape, lambda i: (0, 0, 0)),
            pl.BlockSpec(wp_bf.shape, lambda i: (0, 0, 0)),
            pl.BlockSpec((1, pool_w), lambda i: (0, 0)),
        ],
        out_specs=pl.BlockSpec((b, pool_w), lambda i: (0, 0)),
        out_shape=jax.ShapeDtypeStruct((b, pool_w), BF16),
        compiler_params=pltpu.CompilerParams(vmem_limit_bytes=VMEM_LIMIT),
        name="pool_sample",
    )(proj_s, proj_s, state_t, wp_bf, scale)


def _alibi_slope(head):
    return 2.0 ** (-(head + 1))


def _prompt_attn_kernel(qit_ref, wit_ref, ki_ref, q_ref, k_ref, v_ref, o_ref,
                        keys_ref, m_ref, l_ref, acc_ref, *, n_sel):
    tq, sk = ATT_TQ, ATT_SK
    b = pl.program_id(0)
    t0 = b * tq
    nch = (t0 + tq - 1) // sk + 1

    wit = wit_ref[...]
    sub_s = lax.broadcasted_iota(I32, (IDX_SUB, tq), 0)
    sub_t = lax.broadcasted_iota(I32, (IDX_SUB, tq), 1) + t0

    def score_chunk(c, carry):
        s0 = pl.multiple_of(c * sk, sk)
        for ks in range(sk // IDX_SUB):
            kic = ki_ref[pl.ds(s0 + ks * IDX_SUB, IDX_SUB), :]
            acc = jnp.zeros((IDX_SUB, tq), F32)
            for h in range(N_IDX_HEADS):
                logit = jnp.dot(kic, qit_ref[h], preferred_element_type=F32)
                acc = acc + wit[h:h + 1, :] * jnp.maximum(logit, 0.0)
            key = _sortable_key(acc)
            visible = (sub_s + (s0 + ks * IDX_SUB)) <= sub_t
            keys_ref[c, ks * IDX_SUB:(ks + 1) * IDX_SUB, :] = jnp.where(visible, key, INT_MIN)
        return carry

    lax.fori_loop(0, nch, score_chunk, 0)

    def count_ge(cand):
        def body(c, cnt):
            ind = jnp.where(keys_ref[c] >= cand, 1.0, 0.0)
            return cnt + jnp.sum(ind.reshape(sk // 8, 8, tq), axis=0)

        cnt8 = lax.fori_loop(0, nch, body, jnp.zeros((8, tq), F32))
        return jnp.sum(cnt8, axis=0, keepdims=True)

    need = float(n_sel)
    thr0 = jnp.where(count_ge(jnp.zeros((1, tq), I32)) >= need, 0, INT_MIN).astype(I32)

    def bit_step(i, thr):
        cand = thr | (jnp.int32(1) << (30 - i))
        return jnp.where(count_ge(cand) >= need, cand, thr)

    thr = lax.fori_loop(0, 31, bit_step, thr0)
    thr = jnp.maximum(thr, INT_MIN + 1)

    m_ref[...] = jnp.full(m_ref.shape, -jnp.inf, F32)
    l_ref[...] = jnp.zeros(l_ref.shape, F32)
    acc_ref[...] = jnp.zeros(acc_ref.shape, F32)
    rel = (lax.broadcasted_iota(I32, (tq, sk), 0) - lax.broadcasted_iota(I32, (tq, sk), 1))
    scale = HEAD_DIM ** -0.5
    rep = sk // 128

    def attend_chunk(c, carry):
        s0 = pl.multiple_of(c * sk, sk)
        bias = jnp.where(keys_ref[c] >= thr, 0.0, NEG_BIG).T
        dist = (rel + (t0 - s0)).astype(F32)
        for g in range(N_KV_HEADS):
            kc = k_ref[pl.ds(s0, sk), g * HEAD_DIM:(g + 1) * HEAD_DIM]
            vc = v_ref[pl.ds(s0, sk), g * HEAD_DIM:(g + 1) * HEAD_DIM]
            for r in range(GQA_GROUP):
                h = g * GQA_GROUP + r
                qh = q_ref[:, h * HEAD_DIM:(h + 1) * HEAD_DIM]
                s = lax.dot_general(qh, kc, (((1,), (1,)), ((), ())), preferred_element_type=F32)
                s = s * scale - _alibi_slope(h) * dist + bias
                m_old = m_ref[h]
                m_new = jnp.maximum(m_old, jnp.max(s, axis=1, keepdims=True))
                alpha = jnp.exp(m_old - m_new)
                p = jnp.exp(s - pltpu.repeat(m_new, rep, axis=1))
                l_ref[h] = alpha * l_ref[h] + jnp.sum(p, axis=1, keepdims=True)
                acc_ref[h] = alpha * acc_ref[h] + jnp.dot(p.astype(BF16), vc, preferred_element_type=F32)
                m_ref[h] = m_new
        return carry

    lax.fori_loop(0, nch, attend_chunk, 0)
    for h in range(N_HEADS):
        o_ref[:, h * HEAD_DIM:(h + 1) * HEAD_DIM] = acc_ref[h] / l_ref[h]


def _prompt_attention(qit, wit, ki_bf, q_bf, k_bf, v_bf, n_sel):
    lp = q_bf.shape[0]
    lk = k_bf.shape[0]
    nblk = lp // ATT_TQ
    attn_w = q_bf.shape[1]
    return pl.pallas_call(
        functools.partial(_prompt_attn_kernel, n_sel=n_sel),
        grid=(nblk,),
        in_specs=[
            pl.BlockSpec((N_IDX_HEADS, IDX_DIM, ATT_TQ), lambda b: (0, 0, b)),
            pl.BlockSpec((N_IDX_HEADS, ATT_TQ), lambda b: (0, b)),
            pl.BlockSpec((lk, IDX_DIM), lambda b: (0, 0)),
            pl.BlockSpec((ATT_TQ, attn_w), lambda b: (b, 0)),
            pl.BlockSpec((lk, N_KV_HEADS * HEAD_DIM), lambda b: (0, 0)),
            pl.BlockSpec((lk, N_KV_HEADS * HEAD_DIM), lambda b: (0, 0)),
        ],
        out_specs=pl.BlockSpec((ATT_TQ, attn_w), lambda b: (b, 0)),
        out_shape=jax.ShapeDtypeStruct((lp, attn_w), F32),
        scratch_shapes=[
            pltpu.VMEM((lk // ATT_SK, ATT_SK, ATT_TQ), I32),
            pltpu.VMEM((N_HEADS, ATT_TQ, 128), F32),
            pltpu.VMEM((N_HEADS, ATT_TQ, 128), F32),
            pltpu.VMEM((N_HEADS, ATT_TQ, HEAD_DIM), F32),
        ],
        compiler_params=pltpu.CompilerParams(
            dimension_semantics=("arbitrary",), vmem_limit_bytes=VMEM_LIMIT),
        name="prompt_attn",
    )(qit, wit, ki_bf, q_bf, k_bf, v_bf)


def _sample_score_kernel(pt_ref, qi_ref, wi_ref, kin_ref, *refs):
    del pt_ref
    page_refs, o_ref, onew_ref = refs[:PAGES_PER_STEP], refs[PAGES_PER_STEP], refs[PAGES_PER_STEP + 1]
    qi = qi_ref[0]
    wi = wi_ref[0]

    def page_scores(kpage):
        logit = lax.dot_general(qi, kpage, (((1,), (1,)), ((), ())),
                                precision=lax.Precision.HIGHEST, preferred_element_type=F32)
        return jnp.sum(wi * jnp.maximum(logit, 0.0), axis=0, keepdims=True)

    for i, pr in enumerate(page_refs):
        o_ref[0, :, i * PAGE_SIZE:(i + 1) * PAGE_SIZE] = page_scores(pr[0])
    onew_ref[0] = page_scores(kin_ref[0])


def _sample_scores(page_table, qi_s, wi_s, ki_new_pad, cache_idx):
    nb, n_pages = page_table.shape
    steps = n_pages // PAGES_PER_STEP

    def page_spec(i):
        return pl.BlockSpec((1, PAGE_SIZE, IDX_DIM),
                            lambda b, j, pt: (pt[b, j * PAGES_PER_STEP + i], 0, 0))

    grid_spec = pltpu.PrefetchScalarGridSpec(
        num_scalar_prefetch=1,
        grid=(nb, steps),
        in_specs=[
            pl.BlockSpec((1, N_IDX_HEADS, IDX_DIM), lambda b, j, pt: (b, 0, 0)),
            pl.BlockSpec((1, N_IDX_HEADS, 1), lambda b, j, pt: (b, 0, 0)),
            pl.BlockSpec((1, PAGE_SIZE, IDX_DIM), lambda b, j, pt: (b, 0, 0)),
        ] + [page_spec(i) for i in range(PAGES_PER_STEP)],
        out_specs=[
            pl.BlockSpec((1, 1, PAGES_PER_STEP * PAGE_SIZE), lambda b, j, pt: (b, 0, j)),
            pl.BlockSpec((1, 1, PAGE_SIZE), lambda b, j, pt: (b, 0, 0)),
        ],
    )
    return pl.pallas_call(
        _sample_score_kernel,
        grid_spec=grid_spec,
        out_shape=(jax.ShapeDtypeStruct((nb, 1, n_pages * PAGE_SIZE), F32),
                   jax.ShapeDtypeStruct((nb, 1, PAGE_SIZE), F32)),
        compiler_params=pltpu.CompilerParams(
            dimension_semantics=("arbitrary", "arbitrary"), vmem_limit_bytes=VMEM_LIMIT),
        name="sample_scores",
    )(page_table, qi_s, wi_s, ki_new_pad, *([cache_idx] * PAGES_PER_STEP))


def _sample_threshold_kernel(sc_ref, snew_ref, thr_ref, knew_ref, *, n_sel):
    nb = sc_ref.shape[0]
    keys = _sortable_key(sc_ref[...])
    key_new = _sortable_key(snew_ref[:, 0:1])

    def count_ge(cand):
        cnt = jnp.sum(jnp.where(keys >= cand, 1.0, 0.0), axis=-1, keepdims=True)
        return cnt + jnp.where(key_new >= cand, 1.0, 0.0)

    need = float(n_sel)
    thr0 = jnp.where(count_ge(jnp.zeros((nb, 1), I32)) >= need, 0, INT_MIN).astype(I32)

    def bit_step(i, thr):
        cand = thr | (jnp.int32(1) << (30 - i))
        return jnp.where(count_ge(cand) >= need, cand, thr)

    thr = lax.fori_loop(0, 31, bit_step, thr0)
    thr_ref[...] = jnp.broadcast_to(thr, thr_ref.shape)
    knew_ref[...] = jnp.broadcast_to(key_new, knew_ref.shape)


def _sample_threshold(scores, s_new, n_sel):
    nb = scores.shape[0]
    return pl.pallas_call(
        functools.partial(_sample_threshold_kernel, n_sel=n_sel),
        out_shape=(jax.ShapeDtypeStruct((nb, 128), I32), jax.ShapeDtypeStruct((nb, 128), I32)),
        compiler_params=pltpu.CompilerParams(vmem_limit_bytes=VMEM_LIMIT),
        name="sample_threshold",
    )(scores, s_new)


def _sample_attn_kernel(pt_ref, q_ref, sc_ref, thr_ref, knew_ref, kn_ref, vn_ref, *refs, past):
    del pt_ref
    npg = PAGES_PER_STEP
    k_refs, v_refs = refs[:npg], refs[npg:2 * npg]
    o_ref, m_ref, l_ref, acc_ref = refs[2 * npg:]
    j = pl.program_id(1)
    nsteps = pl.num_programs(1)
    span = npg * PAGE_SIZE
    scale = HEAD_DIM ** -0.5
    row = lax.broadcasted_iota(I32, (N_HEADS, 1), 0)
    slope = lax.bitcast_convert_type((127 - (row + 1)) << 23, F32)
    first_group = row < GQA_GROUP

    @pl.when(j == 0)
    def _():
        m_ref[...] = jnp.full(m_ref.shape, -jnp.inf, F32)
        l_ref[...] = jnp.zeros(l_ref.shape, F32)
        acc_ref[...] = jnp.zeros(acc_ref.shape, F32)

    q = q_ref[0]
    thr = thr_ref[0, :, 0:1]
    sel = _sortable_key(sc_ref[0]) >= thr
    idx = j * span + lax.broadcasted_iota(I32, (1, span), 1)
    dist = (past - idx).astype(F32)
    bias = jnp.where(sel, 0.0, NEG_BIG) - slope * dist

    logits = []
    for kr in k_refs:
        kp = kr[0].astype(BF16)
        l0 = lax.dot_general(q, kp[:, :HEAD_DIM], (((1,), (1,)), ((), ())), preferred_element_type=F32)
        l1 = lax.dot_general(q, kp[:, HEAD_DIM:], (((1,), (1,)), ((), ())), preferred_element_type=F32)
        logits.append(jnp.where(first_group, l0, l1))
    s = jnp.concatenate(logits, axis=1) * scale + bias

    m_old = m_ref[...]
    m_new = jnp.maximum(m_old, jnp.max(s, axis=1, keepdims=True))
    alpha = jnp.exp(m_old - m_new)
    p = jnp.exp(s - m_new[:, 0:1])
    l_ref[...] = alpha * l_ref[...] + jnp.sum(p, axis=1, keepdims=True)
    pv = jnp.zeros((N_HEADS, HEAD_DIM), F32)
    for i, vr in enumerate(v_refs):
        vp = vr[0].astype(BF16)
        o2 = jnp.dot(p[:, i * PAGE_SIZE:(i + 1) * PAGE_SIZE].astype(BF16), vp, preferred_element_type=F32)
        pv = pv + jnp.where(first_group, o2[:, :HEAD_DIM], o2[:, HEAD_DIM:])
    acc_ref[...] = alpha * acc_ref[...] + pv
    m_ref[...] = m_new

    @pl.when(j == nsteps - 1)
    def _():
        kn = kn_ref[0].astype(BF16).astype(F32)
        vn = vn_ref[0].astype(BF16).astype(F32)
        kn8 = jnp.where(first_group, kn[0:1, :], kn[1:2, :])
        vn8 = jnp.where(first_group, vn[0:1, :], vn[1:2, :])
        s_new = jnp.sum(q.astype(F32) * kn8, axis=1, keepdims=True) * scale
        s_new = s_new + jnp.where(knew_ref[0, :, 0:1] >= thr, 0.0, NEG_BIG)
        m_o = m_ref[...]
        m_n = jnp.maximum(m_o, s_new)
        a = jnp.exp(m_o - m_n)
        p_new = jnp.exp(s_new - m_n[:, 0:1])
        l_fin = a * l_ref[...] + p_new
        acc_fin = a * acc_ref[...] + p_new.astype(BF16).astype(F32) * vn8
        o_ref[0] = acc_fin / l_fin


def _sample_attention(page_table, q_s, scores, thr, knew, k_new, v_new, cache_k, cache_v):
    nb, n_pages = page_table.shape
    npg = PAGES_PER_STEP
    steps = n_pages // npg
    kvw = N_KV_HEADS * HEAD_DIM

    def page_spec(i):
        return pl.BlockSpec((1, PAGE_SIZE, kvw), lambda b, j, pt: (pt[b, j * npg + i], 0, 0))

    grid_spec = pltpu.PrefetchScalarGridSpec(
        num_scalar_prefetch=1,
        grid=(nb, steps),
        in_specs=[
            pl.BlockSpec((1, N_HEADS, HEAD_DIM), lambda b, j, pt: (b, 0, 0)),
            pl.BlockSpec((1, 1, npg * PAGE_SIZE), lambda b, j, pt: (b, 0, j)),
            pl.BlockSpec((1, 1, 128), lambda b, j, pt: (b, 0, 0)),
            pl.BlockSpec((1, 1, 128), lambda b, j, pt: (b, 0, 0)),
            pl.BlockSpec((1, N_KV_HEADS, HEAD_DIM), lambda b, j, pt: (b, 0, 0)),
            pl.BlockSpec((1, N_KV_HEADS, HEAD_DIM), lambda b, j, pt: (b, 0, 0)),
        ] + [page_spec(i) for i in range(npg)] * 2,
        out_specs=pl.BlockSpec((1, N_HEADS, HEAD_DIM), lambda b, j, pt: (b, 0, 0)),
        scratch_shapes=[
            pltpu.VMEM((N_HEADS, 128), F32),
            pltpu.VMEM((N_HEADS, 128), F32),
            pltpu.VMEM((N_HEADS, HEAD_DIM), F32),
        ],
    )
    return pl.pallas_call(
        functools.partial(_sample_attn_kernel, past=n_pages * PAGE_SIZE),
        grid_spec=grid_spec,
        out_shape=jax.ShapeDtypeStruct((nb, N_HEADS, HEAD_DIM), F32),
        compiler_params=pltpu.CompilerParams(
            dimension_semantics=("arbitrary", "arbitrary"), vmem_limit_bytes=VMEM_LIMIT),
        name="sample_attn",
    )(page_table, q_s, scores, thr, knew, k_new, v_new, *([cache_k] * npg), *([cache_v] * npg))


def _out_kernel(x_ref, op_ref, oa_ref, za_ref, zb_ref, w_ref, g_ref, o_ref):
    half = za_ref.shape[1]
    pool_w = op_ref.shape[1]
    acc = jnp.dot(op_ref[...], w_ref[0:pool_w, :], preferred_element_type=F32)
    for part, z_ref in enumerate((za_ref, zb_ref)):
        cols = slice(part * half, (part + 1) * half)
        mix = (oa_ref[:, cols] * _silu(z_ref[...])).astype(BF16)
        acc = acc + jnp.dot(mix, w_ref[pool_w + part * half:pool_w + (part + 1) * half, :],
                            preferred_element_type=F32)
    h = x_ref[...] + acc
    ms = jnp.mean(h * h, axis=-1, keepdims=True)
    o_ref[...] = (h * lax.rsqrt(ms + RMS_EPS)) * g_ref[...]


def _merge_norm(x, o_pool, o_attn, proj, za_col_block, w_bf, g, tm):
    m, d = x.shape
    pool_w = o_pool.shape[1]
    attn_w = o_attn.shape[1]
    half = attn_w // 2
    return pl.pallas_call(
        _out_kernel,
        grid=(m // tm,),
        in_specs=[
            pl.BlockSpec((tm, d), lambda i: (i, 0)),
            pl.BlockSpec((tm, pool_w), lambda i: (i, 0)),
            pl.BlockSpec((tm, attn_w), lambda i: (i, 0)),
            pl.BlockSpec((tm, half), lambda i: (i, za_col_block)),
            pl.BlockSpec((tm, half), lambda i: (i, za_col_block + 1)),
            pl.BlockSpec(w_bf.shape, lambda i: (0, 0)),
            pl.BlockSpec((1, d), lambda i: (0, 0)),
        ],
        out_specs=pl.BlockSpec((tm, d), lambda i: (i, 0)),
        out_shape=jax.ShapeDtypeStruct((m, d), F32),
        compiler_params=pltpu.CompilerParams(
            dimension_semantics=("parallel",), vmem_limit_bytes=VMEM_LIMIT),
        name="out_proj",
    )(x, o_pool, o_attn, proj, proj, w_bf, g)


def _round_up(x, m):
    return (x + m - 1) // m * m


def kernel(x_prompt, x_sample, cache_k, cache_v, cache_idx_k, state_pool, page_table, meta_tokens,
           norm_g, w_in, w_pool, pool_scale, w_out, final_norm_g):
    n_b, seq, d_model = x_prompt.shape
    assert n_b == 1 and norm_g.shape[0] == 1 and x_sample.shape[1] == 1
    nb_s = x_sample.shape[0]
    pool_w = pool_scale.shape[1]
    attn_w = N_HEADS * HEAD_DIM
    kvw = N_KV_HEADS * HEAD_DIM
    idx_w = N_IDX_HEADS * IDX_DIM
    c_zp, c_q = pool_w, 2 * pool_w
    c_k, c_v = c_q + attn_w, c_q + attn_w + kvw
    c_za = c_v + kvw
    c_qi = c_za + attn_w
    c_ki = c_qi + idx_w
    c_wi = c_ki + IDX_DIM
    d_in = c_wi + N_IDX_HEADS
    assert w_in.shape[2] == d_in and c_za % (attn_w // 2) == 0

    seq_all = N_META + seq
    lp = _round_up(seq_all, ATT_TQ)
    lk = _round_up(lp, ATT_SK)
    n_proj = _round_up(d_in, PROJ_TN)
    n_pages = page_table.shape[1]
    past = n_pages * PAGE_SIZE

    w_in_bf = jnp.pad(w_in[0].astype(BF16), ((0, 0), (0, n_proj - d_in)))
    w_out_bf = w_out[0].astype(BF16)
    wp_bf = w_pool[0].astype(BF16)
    g_in = norm_g[0][None, :]
    g_fin = final_norm_g[None, :]
    scale_pool = pool_scale[0][None, :]

    h_all = jnp.concatenate(
        [meta_tokens.astype(x_prompt.dtype), x_prompt[0], jnp.zeros((lp - seq_all, d_model), x_prompt.dtype)], axis=0)
    proj = _rmsnorm_matmul(h_all, g_in, w_in_bf, tm=768)

    k_f32 = proj[:seq_all, c_k:c_k + kvw]
    v_f32 = proj[:seq_all, c_v:c_v + kvw]
    ki_f32 = proj[:seq_all, c_ki:c_ki + IDX_DIM]
    pad_k = ((0, lk - seq_all), (0, 0))
    k_bf = jnp.pad(k_f32.astype(BF16), pad_k)
    v_bf = jnp.pad(v_f32.astype(BF16), pad_k)
    ki_bf = jnp.pad(ki_f32.astype(BF16), pad_k)
    q_bf = proj[:, c_q:c_q + attn_w].astype(BF16)
    qit = (proj[:, c_qi:c_qi + idx_w] * (IDX_DIM ** -0.5)).astype(BF16).reshape(lp, N_IDX_HEADS, IDX_DIM)
    qit = qit.transpose(1, 2, 0)
    wit = (proj[:, c_wi:c_wi + N_IDX_HEADS] * (N_IDX_HEADS ** -0.5)).T

    n_sel = max(1, min(TOPK_MAX, seq_all // 4))
    o_pool = _pool_branch(proj, wp_bf, scale_pool, tm=256)
    o_attn = _prompt_attention(qit, wit, ki_bf, q_bf, k_bf, v_bf, n_sel)
    y_all = _merge_norm(h_all, o_pool, o_attn, proj, c_za // (attn_w // 2), w_out_bf, g_fin, tm=256)
    y_prompt = y_all[N_META:seq_all][None]

    xs = x_sample[:, 0, :]
    proj_s = _rmsnorm_matmul(xs, g_in, w_in_bf, tm=nb_s)
    ks_f32 = proj_s[:, c_k:c_k + kvw]
    vs_f32 = proj_s[:, c_v:c_v + kvw]
    q_s = proj_s[:, c_q:c_q + attn_w].astype(BF16).reshape(nb_s, N_HEADS, HEAD_DIM)
    n_idx_cols = d_in - c_qi
    n_idx_pad = _round_up(n_idx_cols, 3 * 128)
    w_idx = jnp.pad(w_in[0][:, c_qi:], ((0, 0), (0, n_idx_pad - n_idx_cols)))
    proj_i = _rmsnorm_matmul_f32(xs, g_in, w_idx, tn=n_idx_pad // 3)
    kis_f32 = proj_i[:, idx_w:idx_w + IDX_DIM]
    qi_s = (proj_i[:, 0:idx_w] * (IDX_DIM ** -0.5)).reshape(nb_s, N_IDX_HEADS, IDX_DIM)
    wi_s = proj_i[:, idx_w + IDX_DIM:idx_w + IDX_DIM + N_IDX_HEADS] * (N_IDX_HEADS ** -0.5)

    state_t = state_pool[0].transpose(1, 0, 2)
    o_pool_s = _pool_branch_sample(proj_s, state_t, wp_bf, scale_pool)

    n_sel_s = max(1, min(TOPK_MAX, (past + 1) // 4))
    cidx = cache_idx_k[0]
    ck = cache_k[0].reshape(cache_k.shape[1], PAGE_SIZE, kvw)
    cv = cache_v[0].reshape(cache_v.shape[1], PAGE_SIZE, kvw)
    ki_new_pad = jnp.pad(kis_f32[:, None, :], ((0, 0), (0, PAGE_SIZE - 1), (0, 0)))
    scores_s, snew_s = _sample_scores(page_table, qi_s, wi_s[:, :, None], ki_new_pad, cidx)
    thr_s, knew_s = _sample_threshold(scores_s[:, 0, :], snew_s[:, 0, :], n_sel_s)
    o_attn_s = _sample_attention(
        page_table, q_s, scores_s, thr_s[:, None, :], knew_s[:, None, :],
        ks_f32.reshape(nb_s, N_KV_HEADS, HEAD_DIM), vs_f32.reshape(nb_s, N_KV_HEADS, HEAD_DIM), ck, cv)
    y_s = _merge_norm(xs, o_pool_s, o_attn_s.reshape(nb_s, attn_w), proj_s, c_za // (attn_w // 2),
                      w_out_bf, g_fin, tm=nb_s)

    new_pool_prompt = proj[seq_all - POOL_PAD:seq_all, 0:pool_w]
    new_pool_sample = jnp.concatenate([state_pool[0][:, 1:, :], proj_s[:, None, 0:pool_w]], axis=1)
    return (
        y_prompt,
        y_s[:, None, :],
        k_f32.reshape(1, 1, seq_all, N_KV_HEADS, HEAD_DIM),
        v_f32.reshape(1, 1, seq_all, N_KV_HEADS, HEAD_DIM),
        ki_f32.reshape(1, 1, seq_all, IDX_DIM),
        new_pool_prompt[None, None],
        ks_f32.reshape(1, nb_s, 1, N_KV_HEADS, HEAD_DIM),
        vs_f32.reshape(1, nb_s, 1, N_KV_HEADS, HEAD_DIM),
        kis_f32.reshape(1, nb_s, 1, IDX_DIM),
        new_pool_sample[None],
    )
```
